```python
import jax, jax.numpy as jnp
from jax import lax
import numpy as np

D_MODEL = 2048
BATCH = 4
SEQ = 4096
DEPTH = 4
DEC_BATCH = 32
DEC_SEQ = 64
PAST_LEN = 2048

CHUNK = 64
Q_BLOCK = 128
N_MIXERS = 2
N_FOX_LAYERS = (DEPTH + 1) // 2
N_MLSTM_LAYERS = DEPTH // 2

FOX_HEADS = 16
FOX_HEAD_DIM = D_MODEL // FOX_HEADS
FOX_IN_COLS = 4 * D_MODEL + FOX_HEADS

MLSTM_HEADS = 8
MLSTM_V_DIM = D_MODEL // MLSTM_HEADS
MLSTM_QK_DIM = MLSTM_V_DIM // 2
MLSTM_QK_WIDTH = MLSTM_HEADS * MLSTM_QK_DIM
MLSTM_IN_COLS = 2 * MLSTM_QK_WIDTH + 2 * D_MODEL + 2 * MLSTM_HEADS

PEER_HEADS = 8
PEER_N_KEYS = 128
PEER_N_EXPERTS = PEER_N_KEYS * PEER_N_KEYS
PEER_TOPK = 16
PEER_KEY_DIM = 256
PEER_HALF = PEER_KEY_DIM // 2
PEER_TOKEN_BLOCK = 128

FORGET_BIAS_INIT = 3.0
EPS = 1e-6

kernel_name = "fox_mlstm_peer_adaln_stream_step"

F32 = jnp.float32


def rmsnorm(x, g):
    xf = x.astype(F32)
    y = xf * lax.rsqrt(jnp.mean(xf * xf, axis=-1, keepdims=True) + EPS)
    return (y * g.astype(F32)).astype(x.dtype)


def modulate(x, g, shift, scale):
    return rmsnorm(x, g) * (1 + scale[:, None, :]) + shift[:, None, :]


def fox_mixer(h, w_in, b_f, g_q, g_k, w_out, k_past, v_past, logf_past):
    B, T, _ = h.shape
    P = k_past.shape[1]
    proj = h @ w_in
    q = rmsnorm(proj[..., 0:D_MODEL].reshape(B, T, FOX_HEADS, FOX_HEAD_DIM), g_q)
    k = rmsnorm(proj[..., D_MODEL:2 * D_MODEL].reshape(B, T, FOX_HEADS, FOX_HEAD_DIM), g_k)
    v = proj[..., 2 * D_MODEL:3 * D_MODEL].reshape(B, T, FOX_HEADS, FOX_HEAD_DIM)
    o_pre = proj[..., 3 * D_MODEL:4 * D_MODEL]
    logf = jax.nn.log_sigmoid((proj[..., 4 * D_MODEL:] + b_f).astype(F32))
    k_all = jnp.concatenate([k_past.astype(k.dtype), k], axis=1)
    v_all = jnp.concatenate([v_past.astype(v.dtype), v], axis=1)
    F = jnp.cumsum(jnp.concatenate([logf_past.astype(F32), logf], axis=1), axis=1)
    F_k = F.transpose(0, 2, 1)
    k_pos = jnp.arange(P + T)
    q_pos = P + jnp.arange(T)
    blk = min(Q_BLOCK, T)
    nb = T // blk
    q_b = q.reshape(B, nb, blk, FOX_HEADS, FOX_HEAD_DIM).transpose(1, 0, 2, 3, 4)
    F_qb = F[:, P:].reshape(B, nb, blk, FOX_HEADS).transpose(1, 0, 3, 2)
    pos_b = q_pos.reshape(nb, blk)
    scale = FOX_HEAD_DIM ** -0.5

    def block(args):
        qi, Fi, pi = args
        s = jnp.einsum('bqhd,bkhd->bhqk', qi, k_all).astype(F32) * scale
        s = s + Fi[..., None] - F_k[:, :, None, :]
        s = jnp.where(k_pos[None, :] <= pi[:, None], s, -jnp.inf)
        p = jax.nn.softmax(s, axis=-1).astype(v_all.dtype)
        return jnp.einsum('bhqk,bkhd->bqhd', p, v_all)

    o = lax.map(block, (q_b, F_qb, pos_b))
    o = o.transpose(1, 0, 2, 3, 4).reshape(B, T, D_MODEL)
    y = (o * jax.nn.sigmoid(o_pre)) @ w_out
    return y, k, v, logf


def mlstm_chunkwise(q, k, v, itil, logf, C0, n0, m0):
    B, T, H, _ = q.shape
    Dv = v.shape[-1]
    L = min(CHUNK, T)
    nc = T // L

    def chunks4(a):
        return a.reshape(B, nc, L, H, a.shape[-1]).transpose(1, 0, 3, 2, 4)

    def chunks3(a):
        return a.reshape(B, nc, L, H).transpose(1, 0, 3, 2)

    causal = jnp.tril(jnp.ones((L, L), dtype=bool))

    def step(carry, xs):
        C, n, m = carry
        qc, kc, vc, ic, fc = xs
        b = jnp.cumsum(fc, axis=-1)
        Dm = jnp.where(causal, b[..., :, None] - b[..., None, :] + ic[..., None, :], -jnp.inf)
        inter = b + m[..., None]
        m_row = jnp.maximum(inter, jnp.max(Dm, axis=-1))
        S = jnp.einsum('bhld,bhsd->bhls', qc, kc) * jnp.exp(Dm - m_row[..., None])
        decay = jnp.exp(inter - m_row)
        num = jnp.einsum('bhls,bhsv->bhlv', S, vc) + decay[..., None] * jnp.einsum('bhld,bhdv->bhlv', qc, C)
        den = jnp.sum(S, axis=-1) + decay * jnp.einsum('bhld,bhd->bhl', qc, n)
        hc = num / jnp.maximum(jnp.abs(den), jnp.exp(-m_row))[..., None]
        m_new = m_row[..., -1]
        w_end = jnp.exp(b[..., -1:] - b + ic - m_new[..., None])
        g = jnp.exp(b[..., -1] + m - m_new)
        C_new = g[..., None, None] * C + jnp.einsum('bhsd,bhs,bhsv->bhdv', kc, w_end, vc)
        n_new = g[..., None] * n + jnp.einsum('bhsd,bhs->bhd', kc, w_end)
        return (C_new, n_new, m_new), hc

    (C, n, m), hs = lax.scan(step, (C0, n0, m0),
                             (chunks4(q), chunks4(k), chunks4(v), chunks3(itil), chunks3(logf)))
    hs = hs.transpose(1, 0, 3, 2, 4).reshape(B, T, H, Dv)
    return hs, C, n, m


def mlstm_mixer(h, w_in, b_i, b_f, g_h, w_out, C0, n0, m0):
    B, T, _ = h.shape
    W = MLSTM_QK_WIDTH
    proj = h @ w_in
    q = proj[..., 0:W].reshape(B, T, MLSTM_HEADS, MLSTM_QK_DIM)
    k = proj[..., W:2 * W].reshape(B, T, MLSTM_HEADS, MLSTM_QK_DIM) * (MLSTM_QK_DIM ** -0.5)
    v = proj[..., 2 * W:2 * W + D_MODEL].reshape(B, T, MLSTM_HEADS, MLSTM_V_DIM)
    o_pre = proj[..., 2 * W + D_MODEL:2 * W + 2 * D_MODEL]
    i_pre = proj[..., 2 * W + 2 * D_MODEL:2 * W + 2 * D_MODEL + MLSTM_HEADS]
    f_pre = proj[..., 2 * W + 2 * D_MODEL + MLSTM_HEADS:]
    itil = (i_pre + b_i).astype(F32)
    logf = jax.nn.log_sigmoid((f_pre + b_f).astype(F32))
    hs, C, n, m = mlstm_chunkwise(q.astype(F32), k.astype(F32), v.astype(F32), itil, logf,
                                  C0.astype(F32), n0.astype(F32), m0.astype(F32))
    hs = rmsnorm(hs, g_h.reshape(MLSTM_HEADS, MLSTM_V_DIM)).reshape(B, T, D_MODEL).astype(h.dtype)
    y = (hs * jax.nn.sigmoid(o_pre)) @ w_out
    return y, C, n, m


def peer_ffn(h, w_query, sub_keys, u, v):
    B, T, D = h.shape
    n = B * T
    blk = PEER_TOKEN_BLOCK
    n_pad = -(-n // blk) * blk
    xt = jnp.pad(h.reshape(n, D), ((0, n_pad - n), (0, 0))).reshape(n_pad // blk, blk, D)
    K = PEER_TOPK

    def block(xb):
        qr = (xb @ w_query).reshape(blk, PEER_HEADS, 2, PEER_HALF)
        s = jnp.einsum('tpcd,pckd->tpck', qr, sub_keys).astype(F32)
        s1, i1 = lax.top_k(s[:, :, 0], K)
        s2, i2 = lax.top_k(s[:, :, 1], K)
        cand = (s1[..., :, None] + s2[..., None, :]).reshape(blk, PEER_HEADS, K * K)
        sc, ci = lax.top_k(cand, K)
        e = (jnp.take_along_axis(i1, ci // K, axis=-1) * PEER_N_KEYS
             + jnp.take_along_axis(i2, ci % K, axis=-1))
        g = jax.nn.softmax(sc, axis=-1)
        a = jax.nn.gelu(jnp.einsum('tpkd,td->tpk', u[e], xb).astype(F32))
        return jnp.einsum('tpk,tpkd->td', (g * a).astype(xb.dtype), v[e])

    out = lax.map(block, xt).reshape(n_pad, D)[:n]
    return out.reshape(B, T, D)


def trunk(x, c, fox_past, mlstm_past, p):
    new_fox = []
    new_rec = []
    for i in range(DEPTH):
        mod = jax.nn.silu(c) @ p['w_ada'][i] + p['b_ada'][i]
        sh1, sc1, ga1, sh2, sc2, ga2 = jnp.split(mod, 6, axis=-1)
        hm = modulate(x, p['g_norm_mix'][i], sh1, sc1)
        j = i // N_MIXERS
        if i % N_MIXERS == 0:
            kp, vp, lp = fox_past[j]
            y, k, v, lf = fox_mixer(hm, p['w_in_fox'][j], p['b_f_fox'][j], p['g_q_fox'][j],
                                    p['g_k_fox'][j], p['w_out_fox'][j], kp, vp, lp)
            new_fox.append((k, v, lf))
        else:
            C0, n0, m0 = mlstm_past[j]
            y, C, n, m = mlstm_mixer(hm, p['w_in_mlstm'][j], p['b_i_mlstm'][j], p['b_f_mlstm'][j],
                                     p['g_h_mlstm'][j], p['w_out_mlstm'][j], C0, n0, m0)
            new_rec.append((C, n, m))
        x = x + ga1[:, None, :] * y
        hf = modulate(x, p['g_norm_ffn'][i], sh2, sc2)
        x = x + ga2[:, None, :] * peer_ffn(hf, p['w_query_peer'][i], p['sub_keys_peer'][i],
                                           p['u_peer'][i], p['v_peer'][i])
    return rmsnorm(x, p['g_final']), new_fox, new_rec


def setup_inputs(seed: int = 0) -> dict:
    key = jax.random.key(seed)
    ks = iter(jax.random.split(key, 40))

    def nrm(shape, s=1.0):
        return jax.random.normal(next(ks), shape, F32) * s

    inv = D_MODEL ** -0.5
    return {
        "x_prompt": nrm((BATCH, SEQ, D_MODEL)),
        "x_sample": nrm((DEC_BATCH, DEC_SEQ, D_MODEL)),
        "c_prompt": nrm((BATCH, D_MODEL)),
        "c_sample": nrm((DEC_BATCH, D_MODEL)),
        "cache_fox_k": nrm((N_FOX_LAYERS, DEC_BATCH, PAST_LEN, FOX_HEADS, FOX_HEAD_DIM)),
        "cache_fox_v": nrm((N_FOX_LAYERS, DEC_BATCH, PAST_LEN, FOX_HEADS, FOX_HEAD_DIM)),
        "cache_fox_logf": jax.nn.log_sigmoid(FORGET_BIAS_INIT + nrm((N_FOX_LAYERS, DEC_BATCH, PAST_LEN, FOX_HEADS))),
        "state_mlstm_C": nrm((N_MLSTM_LAYERS, DEC_BATCH, MLSTM_HEADS, MLSTM_QK_DIM, MLSTM_V_DIM), 0.1),
        "state_mlstm_n": nrm((N_MLSTM_LAYERS, DEC_BATCH, MLSTM_HEADS, MLSTM_QK_DIM), 0.1),
        "state_mlstm_m": nrm((N_MLSTM_LAYERS, DEC_BATCH, MLSTM_HEADS)),
        "w_ada": nrm((DEPTH, D_MODEL, 6 * D_MODEL), 0.5 * inv),
        "b_ada": nrm((DEPTH, 6 * D_MODEL), 0.02),
        "g_norm_mix": 1.0 + nrm((DEPTH, D_MODEL), 0.02),
        "g_norm_ffn": 1.0 + nrm((DEPTH, D_MODEL), 0.02),
        "w_in_fox": nrm((N_FOX_LAYERS, D_MODEL, FOX_IN_COLS), inv),
        "b_f_fox": FORGET_BIAS_INIT + nrm((N_FOX_LAYERS, FOX_HEADS), 0.1),
        "g_q_fox": 1.0 + nrm((N_FOX_LAYERS, FOX_HEAD_DIM), 0.02),
        "g_k_fox": 1.0 + nrm((N_FOX_LAYERS, FOX_HEAD_DIM), 0.02),
        "w_out_fox": nrm((N_FOX_LAYERS, D_MODEL, D_MODEL), inv),
        "w_in_mlstm": nrm((N_MLSTM_LAYERS, D_MODEL, MLSTM_IN_COLS), inv),
        "b_i_mlstm": nrm((N_MLSTM_LAYERS, MLSTM_HEADS), 0.1),
        "b_f_mlstm": FORGET_BIAS_INIT + nrm((N_MLSTM_LAYERS, MLSTM_HEADS), 0.1),
        "g_h_mlstm": 1.0 + nrm((N_MLSTM_LAYERS, D_MODEL), 0.02),
        "w_out_mlstm": nrm((N_MLSTM_LAYERS, D_MODEL, D_MODEL), inv),
        "w_query_peer": nrm((DEPTH, D_MODEL, PEER_HEADS * PEER_KEY_DIM), inv),
        "sub_keys_peer": nrm((DEPTH, PEER_HEADS, 2, PEER_N_KEYS, PEER_HALF), PEER_HALF ** -0.5),
        "u_peer": nrm((DEPTH, PEER_N_EXPERTS, D_MODEL), inv),
        "v_peer": nrm((DEPTH, PEER_N_EXPERTS, D_MODEL), PEER_HEADS ** -0.5),
        "g_final": 1.0 + nrm((D_MODEL,), 0.02),
    }


def reference(x_prompt, x_sample, c_prompt, c_sample, cache_fox_k, cache_fox_v, cache_fox_logf,
              state_mlstm_C, state_mlstm_n, state_mlstm_m, w_ada, b_ada, g_norm_mix, g_norm_ffn,
              w_in_fox, b_f_fox, g_q_fox, g_k_fox, w_out_fox, w_in_mlstm, b_i_mlstm, b_f_mlstm,
              g_h_mlstm, w_out_mlstm, w_query_peer, sub_keys_peer, u_peer, v_peer, g_final):
    p = dict(w_ada=w_ada, b_ada=b_ada, g_norm_mix=g_norm_mix, g_norm_ffn=g_norm_ffn,
             w_in_fox=w_in_fox, b_f_fox=b_f_fox, g_q_fox=g_q_fox, g_k_fox=g_k_fox, w_out_fox=w_out_fox,
             w_in_mlstm=w_in_mlstm, b_i_mlstm=b_i_mlstm, b_f_mlstm=b_f_mlstm, g_h_mlstm=g_h_mlstm,
             w_out_mlstm=w_out_mlstm, w_query_peer=w_query_peer, sub_keys_peer=sub_keys_peer,
             u_peer=u_peer, v_peer=v_peer, g_final=g_final)

    Bp = x_prompt.shape[0]
    fox_empty = [(jnp.zeros((Bp, 0, FOX_HEADS, FOX_HEAD_DIM), x_prompt.dtype),
                  jnp.zeros((Bp, 0, FOX_HEADS, FOX_HEAD_DIM), x_prompt.dtype),
                  jnp.zeros((Bp, 0, FOX_HEADS), F32)) for _ in range(N_FOX_LAYERS)]
    rec_zero = [(jnp.zeros((Bp, MLSTM_HEADS, MLSTM_QK_DIM, MLSTM_V_DIM), F32),
                 jnp.zeros((Bp, MLSTM_HEADS, MLSTM_QK_DIM), F32),
                 jnp.zeros((Bp, MLSTM_HEADS), F32)) for _ in range(N_MLSTM_LAYERS)]
    y_prompt, fox_p, rec_p = trunk(x_prompt, c_prompt, fox_empty, rec_zero, p)

    fox_past = [(cache_fox_k[j], cache_fox_v[j], cache_fox_logf[j]) for j in range(N_FOX_LAYERS)]
    rec_past = [(state_mlstm_C[j], state_mlstm_n[j], state_mlstm_m[j]) for j in range(N_MLSTM_LAYERS)]
    y_sample, fox_s, rec_s = trunk(x_sample, c_sample, fox_past, rec_past, p)

    fox_k_prompt = jnp.stack([t[0] for t in fox_p])
    fox_v_prompt = jnp.stack([t[1] for t in fox_p])
    fox_logf_prompt = jnp.stack([t[2] for t in fox_p])
    mlstm_C_prompt = jnp.stack([t[0] for t in rec_p])
    mlstm_n_prompt = jnp.stack([t[1] for t in rec_p])
    mlstm_m_prompt = jnp.stack([t[2] for t in rec_p])
    fox_k_sample = jnp.stack([t[0] for t in fox_s])
    fox_v_sample = jnp.stack([t[1] for t in fox_s])
    fox_logf_sample = jnp.stack([t[2] for t in fox_s])
    mlstm_C_sample = jnp.stack([t[0] for t in rec_s])
    mlstm_n_sample = jnp.stack([t[1] for t in rec_s])
    mlstm_m_sample = jnp.stack([t[2] for t in rec_s])
    return (y_prompt, y_sample, fox_k_prompt, fox_v_prompt, fox_logf_prompt, mlstm_C_prompt,
            mlstm_n_prompt, mlstm_m_prompt, fox_k_sample, fox_v_sample, fox_logf_sample,
            mlstm_C_sample, mlstm_n_sample, mlstm_m_sample)
```

```python
import functools
import math

import jax
import jax.numpy as jnp
from jax import lax
from jax.experimental import pallas as pl
from jax.experimental.pallas import tpu as pltpu

F32 = jnp.float32
BF16 = jnp.bfloat16
EPS = 1e-6
PEER_TOPK = 16
MLSTM_CHUNK = 64
LANES = 128
VMEM_LIMIT = 56 * 1024 * 1024
HIGHEST = lax.Precision.HIGHEST
NEG_INF = float("-inf")

_NT = (((1,), (1,)), ((), ()))
_TN = (((0,), (0,)), ((), ()))


def _params(*sem):
    return pltpu.CompilerParams(dimension_semantics=sem, vmem_limit_bytes=VMEM_LIMIT)


def _log_sigmoid(z):
    return jnp.minimum(z, 0.0) - jnp.log(1.0 + jnp.exp(-jnp.abs(z)))


def _gelu_tanh(a):
    c = math.sqrt(2.0 / math.pi)
    return 0.5 * a * (1.0 + jnp.tanh(c * (a + 0.044715 * a * a * a)))


def _ada_kernel(c_ref, w_ref, b_ref, o_ref):
    c = c_ref[...]
    s = c * jax.nn.sigmoid(c)
    o_ref[0] = jnp.dot(s, w_ref[0], precision=HIGHEST, preferred_element_type=F32) + b_ref[0]


def _ada(c_all, w_ada, b_ada):
    depth, d, n6 = w_ada.shape
    r = c_all.shape[0]
    tn = min(1024, n6)
    return pl.pallas_call(
        _ada_kernel,
        grid=(depth, n6 // tn),
        in_specs=[pl.BlockSpec((r, d), lambda l, j: (0, 0)),
                  pl.BlockSpec((1, d, tn), lambda l, j: (l, 0, j)),
                  pl.BlockSpec((1, 1, tn), lambda l, j: (l, 0, j))],
        out_specs=pl.BlockSpec((1, r, tn), lambda l, j: (l, 0, j)),
        out_shape=jax.ShapeDtypeStruct((depth, r, n6), F32),
        compiler_params=_params("parallel", "parallel"),
        name="ada",
    )(c_all, w_ada, b_ada.reshape(depth, 1, n6))


def _norm_proj_kernel(*refs, mode, group, n_norm, head_dim, has_gates, emit_h):
    x_ref, g_ref, sh_ref, sc_ref, w_ref = refs[:5]
    pos = 5
    wg_ref = gvec_ref = None
    if has_gates:
        wg_ref = refs[pos]; pos += 1
    if mode != "plain":
        gvec_ref = refs[pos]; pos += 1
    o_ref = refs[pos]; pos += 1
    gate_ref = hout_ref = None
    if has_gates:
        gate_ref = refs[pos]; pos += 1
    if emit_h:
        hout_ref = refs[pos]; pos += 1
    hs_ref = refs[pos]

    j = pl.program_id(1)
    tm = x_ref.shape[0]

    @pl.when(j == 0)
    def _():
        g = g_ref[...]
        for r in range(tm // group):
            rows = slice(r * group, (r + 1) * group)
            x = x_ref[rows, :]
            y = x * lax.rsqrt(jnp.mean(x * x, axis=-1, keepdims=True) + EPS) * g
            h = y * (1.0 + sc_ref[r:r + 1, :]) + sh_ref[r:r + 1, :]
            hs_ref[rows, :] = h.astype(BF16)
            if emit_h:
                hout_ref[rows, :] = h
            if has_gates:
                gate_ref[rows, :] = jnp.dot(h, wg_ref[...], precision=HIGHEST,
                                            preferred_element_type=F32)

    acc = jnp.dot(hs_ref[...], w_ref[...].astype(BF16), preferred_element_type=F32)
    if mode == "plain":
        o_ref[...] = acc
    elif mode == "scale":
        o_ref[...] = acc * gvec_ref[...]
    else:
        @pl.when(j < n_norm)
        def _():
            for hh in range(acc.shape[1] // head_dim):
                cs = slice(hh * head_dim, (hh + 1) * head_dim)
                seg = acc[:, cs]
                ms = jnp.mean(seg * seg, axis=-1, keepdims=True)
                o_ref[:, cs] = seg * lax.rsqrt(ms + EPS) * gvec_ref[:, cs]

        @pl.when(j >= n_norm)
        def _():
            o_ref[...] = acc


def _norm_proj(x, g, mod_g, shift_blk, scale_blk, w, *, group, tm, tn, mode,
               wg=None, gvec=None, n_norm=0, head_dim=LANES, ncols=None):
    n, d = x.shape
    ncols = w.shape[1] if ncols is None else ncols
    has_gates = wg is not None
    emit_h = mode == "plain"
    gr = tm // group
    in_specs = [pl.BlockSpec((tm, d), lambda i, j: (i, 0)),
                pl.BlockSpec((1, d), lambda i, j: (0, 0)),
                pl.BlockSpec((gr, d), lambda i, j: (i, shift_blk)),
                pl.BlockSpec((gr, d), lambda i, j: (i, scale_blk)),
                pl.BlockSpec((d, tn), lambda i, j: (0, j))]
    args = [x, g.reshape(1, d), mod_g, mod_g, w]
    if has_gates:
        in_specs.append(pl.BlockSpec((d, LANES), lambda i, j: (0, 0)))
        args.append(wg)
    if mode != "plain":
        in_specs.append(pl.BlockSpec((1, tn), lambda i, j: (0, j)))
        args.append(gvec)
    out_specs = [pl.BlockSpec((tm, tn), lambda i, j: (i, j))]
    out_shape = [jax.ShapeDtypeStruct((n, ncols), F32)]
    if has_gates:
        out_specs.append(pl.BlockSpec((tm, LANES), lambda i, j: (i, 0)))
        out_shape.append(jax.ShapeDtypeStruct((n, LANES), F32))
    if emit_h:
        out_specs.append(pl.BlockSpec((tm, d), lambda i, j: (i, 0)))
        out_shape.append(jax.ShapeDtypeStruct((n, d), F32))
    kern = functools.partial(_norm_proj_kernel, mode=mode, group=group, n_norm=n_norm,
                             head_dim=head_dim, has_gates=has_gates, emit_h=emit_h)
    return pl.pallas_call(
        kern, grid=(n // tm, ncols // tn), in_specs=in_specs, out_specs=out_specs,
        out_shape=out_shape, scratch_shapes=[pltpu.VMEM((tm, d), BF16)],
        compiler_params=_params("parallel", "arbitrary"), name="norm_proj_" + mode,
    )(*args)


def _fox_gate_kernel(*refs, p_len, t_len, ck_p, ck_t):
    if p_len:
        z_ref, b_ref, past_ref, logf_ref, f_ref = refs
    else:
        z_ref, b_ref, logf_ref, f_ref = refs
        past_ref = None

    def tri(ck):
        r = lax.broadcasted_iota(jnp.int32, (ck, ck), 0)
        c = lax.broadcasted_iota(jnp.int32, (ck, ck), 1)
        return (r >= c).astype(F32)

    carry = jnp.zeros((1, LANES), F32)
    if p_len:
        tp = tri(ck_p)
        for c in range(p_len // ck_p):
            rows = slice(c * ck_p, (c + 1) * ck_p)
            fc = jnp.dot(tp, past_ref[0, rows, :], precision=HIGHEST,
                         preferred_element_type=F32) + carry
            f_ref[0, rows, :] = fc
            carry = fc[ck_p - 1:ck_p, :]
    tt = tri(ck_t)
    for c in range(t_len // ck_t):
        rows = slice(c * ck_t, (c + 1) * ck_t)
        lf = _log_sigmoid(z_ref[rows, :] + b_ref[...])
        logf_ref[0, rows, :] = lf
        fc = jnp.dot(tt, lf, precision=HIGHEST, preferred_element_type=F32) + carry
        f_ref[0, p_len + c * ck_t:p_len + (c + 1) * ck_t, :] = fc
        carry = fc[ck_t - 1:ck_t, :]


def _fox_gates(gates, b_pad, past, *, nb, t_len, row0):
    p_len = 0 if past is None else past.shape[1]
    ck_t = min(256, t_len)
    ck_p = min(256, p_len) if p_len else 0
    blk0 = row0 // t_len
    in_specs = [pl.BlockSpec((t_len, LANES), lambda b: (blk0 + b, 0)),
                pl.BlockSpec((1, LANES), lambda b: (0, 0))]
    args = [gates, b_pad]
    if p_len:
        in_specs.append(pl.BlockSpec((1, p_len, LANES), lambda b: (b, 0, 0)))
        args.append(past)
    kern = functools.partial(_fox_gate_kernel, p_len=p_len, t_len=t_len, ck_p=ck_p, ck_t=ck_t)
    return pl.pallas_call(
        kern, grid=(nb,), in_specs=in_specs,
        out_specs=[pl.BlockSpec((1, t_len, LANES), lambda b: (b, 0, 0)),
                   pl.BlockSpec((1, p_len + t_len, LANES), lambda b: (b, 0, 0))],
        out_shape=[jax.ShapeDtypeStruct((nb, t_len, LANES), F32),
                   jax.ShapeDtypeStruct((nb, p_len + t_len, LANES), F32)],
        compiler_params=_params("parallel"), name="fox_gates",
    )(*args)


def _flash_update(s, v, m_prev, l_prev, acc_prev):
    m_new = jnp.maximum(m_prev, jnp.max(s, axis=-1, keepdims=True))
    p = jnp.exp(s - m_new)
    alpha = jnp.exp(m_prev - m_new)
    l_new = alpha * l_prev + jnp.sum(p, axis=-1, keepdims=True)
    acc_new = alpha * acc_prev + jnp.dot(p.astype(BF16), v, preferred_element_type=F32)
    return m_new, l_new, acc_new


def _fox_attn_prompt_kernel(q_ref, k_ref, v_ref, f_ref, oin_ref, o_ref, *, scale, tq):
    del oin_ref
    qi = pl.program_id(2)
    q = (q_ref[...] * scale).astype(BF16)
    dh = q.shape[1]

    def step(ki, carry, masked):
        m, l, acc = carry
        start = pl.multiple_of(ki * tq, tq)
        k = k_ref[pl.ds(start, tq), :].astype(BF16)
        v = v_ref[pl.ds(start, tq), :].astype(BF16)
        s = lax.dot_general(q, k, _NT, preferred_element_type=F32)
        s = s - f_ref[0, 0, ki]
        if masked:
            r = lax.broadcasted_iota(jnp.int32, s.shape, 0)
            c = lax.broadcasted_iota(jnp.int32, s.shape, 1)
            s = jnp.where(c <= r, s, NEG_INF)
        return _flash_update(s, v, m, l, acc)

    init = (jnp.full((tq, 1), NEG_INF, F32), jnp.zeros((tq, 1), F32), jnp.zeros((tq, dh), F32))
    carry = lax.fori_loop(0, qi, lambda ki, c: step(ki, c, False), init)
    m, l, acc = step(qi, carry, True)
    o_ref[...] = acc / l


def _fox_attn_prompt(proj, f_rows, n_total, *, nb, t_len, heads, dh, tq):
    d = heads * dh
    nq = t_len // tq
    f_rows = f_rows.reshape(nb, heads, nq, 1, tq)
    kern = functools.partial(_fox_attn_prompt_kernel, scale=dh ** -0.5, tq=tq)
    return pl.pallas_call(
        kern, grid=(nb, heads, nq),
        in_specs=[pl.BlockSpec((tq, dh), lambda b, h, i: (b * nq + i, h)),
                  pl.BlockSpec((t_len, dh), lambda b, h, i: (b, heads + h)),
                  pl.BlockSpec((t_len, dh), lambda b, h, i: (b, 2 * heads + h)),
                  pl.BlockSpec((1, 1, nq, 1, tq), lambda b, h, i: (b, h, 0, 0, 0)),
                  pl.BlockSpec(memory_space=pl.ANY)],
        out_specs=pl.BlockSpec((tq, dh), lambda b, h, i: (b * nq + i, h)),
        out_shape=jax.ShapeDtypeStruct((n_total, d), F32),
        input_output_aliases={4: 0},
        compiler_params=_params("parallel", "parallel", "arbitrary"), name="fox_attn_prompt",
    )(proj, proj, proj, f_rows, jnp.zeros((n_total, d), F32))


def _fox_attn_sample_kernel(q_ref, kn_ref, vn_ref, kp_ref, vp_ref, fp_ref, fn_ref, oin_ref,
                            o_ref, m_sc, l_sc, acc_sc, *, scale, heads, dh):
    del oin_ref
    kb = pl.program_id(1)

    @pl.when(kb == 0)
    def _():
        m_sc[...] = jnp.full(m_sc.shape, NEG_INF, F32)
        l_sc[...] = jnp.zeros(l_sc.shape, F32)
        acc_sc[...] = jnp.zeros(acc_sc.shape, F32)

    def update(h, s, v):
        m, l, acc = _flash_update(s, v, m_sc[h], l_sc[h], acc_sc[h])
        m_sc[h] = m
        l_sc[h] = l
        acc_sc[h] = acc

    for h in range(heads):
        cs = slice(h * dh, (h + 1) * dh)
        q = (q_ref[:, cs] * scale).astype(BF16)
        k = kp_ref[0, 0, :, cs].astype(BF16)
        v = vp_ref[0, 0, :, cs].astype(BF16)
        s = lax.dot_general(q, k, _NT, preferred_element_type=F32) - fp_ref[0, h]
        update(h, s, v)

    @pl.when(kb == pl.num_programs(1) - 1)
    def _():
        for h in range(heads):
            cs = slice(h * dh, (h + 1) * dh)
            q = (q_ref[:, cs] * scale).astype(BF16)
            k = kn_ref[:, cs].astype(BF16)
            v = vn_ref[:, cs].astype(BF16)
            s = lax.dot_general(q, k, _NT, preferred_element_type=F32) - fn_ref[0, h]
            r = lax.broadcasted_iota(jnp.int32, s.shape, 0)
            c = lax.broadcasted_iota(jnp.int32, s.shape, 1)
            update(h, jnp.where(c <= r, s, NEG_INF), v)
            o_ref[:, cs] = acc_sc[h] / l_sc[h]


def _fox_attn_sample(proj, cache_k, cache_v, layer, fp_rows, fn_rows, o_buf, *, nb, t_len,
                     row0, heads, dh, tk):
    d = heads * dh
    p_len = cache_k.shape[2]
    blk0 = row0 // t_len
    kern = functools.partial(_fox_attn_sample_kernel, scale=dh ** -0.5, heads=heads, dh=dh)
    return pl.pallas_call(
        kern, grid=(nb, p_len // tk),
        in_specs=[pl.BlockSpec((t_len, d), lambda b, kb: (blk0 + b, 0)),
                  pl.BlockSpec((t_len, d), lambda b, kb: (blk0 + b, 1)),
                  pl.BlockSpec((t_len, d), lambda b, kb: (blk0 + b, 2)),
                  pl.BlockSpec((1, 1, tk, d), lambda b, kb: (layer, b, kb, 0)),
                  pl.BlockSpec((1, 1, tk, d), lambda b, kb: (layer, b, kb, 0)),
                  pl.BlockSpec((1, heads, 1, tk), lambda b, kb: (b, 0, 0, kb)),
                  pl.BlockSpec((1, heads, 1, t_len), lambda b, kb: (b, 0, 0, 0)),
                  pl.BlockSpec(memory_space=pl.ANY)],
        out_specs=pl.BlockSpec((t_len, d), lambda b, kb: (blk0 + b, 0)),
        out_shape=jax.ShapeDtypeStruct(o_buf.shape, F32),
        scratch_shapes=[pltpu.VMEM((heads, t_len, 1), F32), pltpu.VMEM((heads, t_len, 1), F32),
                        pltpu.VMEM((heads, t_len, dh), F32)],
        input_output_aliases={7: 0},
        compiler_params=_params("parallel", "arbitrary"), name="fox_attn_sample",
    )(proj, proj, proj, cache_k, cache_v, fp_rows, fn_rows, o_buf)


def _out_proj_kernel(o_ref, op_ref, w_ref, x_ref, ga_ref, y_ref, lhs_ref, *, group):
    @pl.when(pl.program_id(1) == 0)
    def _():
        lhs_ref[...] = (o_ref[...] * jax.nn.sigmoid(op_ref[...])).astype(BF16)

    y = jnp.dot(lhs_ref[...], w_ref[...].astype(BF16), preferred_element_type=F32)
    tm = y.shape[0]
    for r in range(tm // group):
        rows = slice(r * group, (r + 1) * group)
        y_ref[rows, :] = x_ref[rows, :] + ga_ref[r:r + 1, :] * y[rows, :]


def _out_proj(o, proj, opre_blk, w, x, mod_g, gate_blk, *, group, tm, tn):
    n, d = x.shape
    gr = tm // group
    nj = d // tn
    return pl.pallas_call(
        functools.partial(_out_proj_kernel, group=group), grid=(n // tm, nj),
        in_specs=[pl.BlockSpec((tm, d), lambda i, j: (i, 0)),
                  pl.BlockSpec((tm, d), lambda i, j: (i, opre_blk)),
                  pl.BlockSpec((d, tn), lambda i, j: (0, j)),
                  pl.BlockSpec((tm, tn), lambda i, j: (i, j)),
                  pl.BlockSpec((gr, tn), lambda i, j: (i, gate_blk * nj + j))],
        out_specs=pl.BlockSpec((tm, tn), lambda i, j: (i, j)),
        out_shape=jax.ShapeDtypeStruct((n, d), F32),
        scratch_shapes=[pltpu.VMEM((tm, d), BF16)],
        compiler_params=_params("parallel", "arbitrary"), name="out_proj",
    )(o, proj, w, x, mod_g)


def _mlstm_kernel(q_ref, k_ref, v_ref, gt_ref, bias_ref, c0_ref, n0_ref, m0_ref, gh_ref, hin_ref,
                  hs_ref, c_ref, n_ref, m_ref, *, heads, dk, dv, chunk):
    del hin_ref

    @pl.when(pl.program_id(1) == 0)
    def _():
        c_ref[...] = c0_ref[...]
        n_ref[...] = n0_ref[...]
        m_ref[...] = m0_ref[...]

    z = gt_ref[...] + bias_ref[...]
    r = lax.broadcasted_iota(jnp.int32, (chunk, chunk), 0)
    c = lax.broadcasted_iota(jnp.int32, (chunk, chunk), 1)
    causal = r >= c
    bcum = jnp.dot(causal.astype(F32), _log_sigmoid(z), precision=HIGHEST,
                   preferred_element_type=F32)
    z_t = z.T
    bcum_t = bcum.T
    for h in range(heads):
        bcol = bcum[:, heads + h:heads + h + 1]
        brow = bcum_t[heads + h:heads + h + 1, :]
        icol = z[:, h:h + 1]
        irow = z_t[h:h + 1, :]
        m_prev = m_ref[0, h][:, :1]
        dm = jnp.where(causal, bcol - brow + irow, NEG_INF)
        inter = bcol + m_prev
        m_row = jnp.maximum(inter, jnp.max(dm, axis=-1, keepdims=True))
        qh = q_ref[:, h * dk:(h + 1) * dk]
        kh = k_ref[:, h * dk:(h + 1) * dk]
        vh = v_ref[:, h * dv:(h + 1) * dv].astype(BF16)
        qb = qh.astype(BF16)
        s = lax.dot_general(qb, kh.astype(BF16), _NT, preferred_element_type=F32) * jnp.exp(dm - m_row)
        decay = jnp.exp(inter - m_row)
        c_old = c_ref[0, h]
        n_old = n_ref[0, h]
        num = (jnp.dot(s.astype(BF16), vh, preferred_element_type=F32)
               + decay * jnp.dot(qb, c_old.astype(BF16), preferred_element_type=F32))
        den = (jnp.sum(s, axis=-1, keepdims=True)
               + decay * jnp.sum(qb.astype(F32) * n_old.astype(BF16).astype(F32), axis=-1, keepdims=True))
        hc = num / jnp.maximum(jnp.abs(den), jnp.exp(-m_row))
        blast = bcol[chunk - 1:chunk, :]
        m_new = m_row[chunk - 1:chunk, :]
        wend = jnp.exp(blast - bcol + icol - m_new)
        g = jnp.exp(blast + m_prev - m_new)
        kw = kh * wend
        c_ref[0, h] = g * c_old + lax.dot_general(kw.astype(BF16), vh, _TN, preferred_element_type=F32)
        n_ref[0, h] = g * n_old + jnp.sum(kw, axis=0, keepdims=True)
        m_ref[0, h] = jnp.broadcast_to(m_new, (1, LANES))
        hn = hc * lax.rsqrt(jnp.mean(hc * hc, axis=-1, keepdims=True) + EPS)
        hs_ref[:, h * dv:(h + 1) * dv] = hn * gh_ref[:, h * dv:(h + 1) * dv]


def _mlstm(proj, gates, bias, c0, n0, m0, gh, h_buf, *, nb, t_len, row0, b0, heads, dk, dv):
    d = heads * dv
    chunk = min(MLSTM_CHUNK, t_len)
    nc = t_len // chunk
    blk0 = row0 // chunk
    qk = heads * dk
    vblk = 2 * qk // d
    kern = functools.partial(_mlstm_kernel, heads=heads, dk=dk, dv=dv, chunk=chunk)
    row = lambda b, c: blk0 + b * nc + c
    return pl.pallas_call(
        kern, grid=(nb, nc),
        in_specs=[pl.BlockSpec((chunk, qk), lambda b, c: (row(b, c), 0)),
                  pl.BlockSpec((chunk, qk), lambda b, c: (row(b, c), 1)),
                  pl.BlockSpec((chunk, d), lambda b, c: (row(b, c), vblk)),
                  pl.BlockSpec((chunk, LANES), lambda b, c: (row(b, c), 0)),
                  pl.BlockSpec((1, LANES), lambda b, c: (0, 0)),
                  pl.BlockSpec((1, heads, dk, dv), lambda b, c: (b0 + b, 0, 0, 0)),
                  pl.BlockSpec((1, heads, 1, dk), lambda b, c: (b0 + b, 0, 0, 0)),
                  pl.BlockSpec((1, heads, 1, LANES), lambda b, c: (b0 + b, 0, 0, 0)),
                  pl.BlockSpec((1, d), lambda b, c: (0, 0)),
                  pl.BlockSpec(memory_space=pl.ANY)],
        out_specs=[pl.BlockSpec((chunk, d), lambda b, c: (row(b, c), 0)),
                   pl.BlockSpec((1, heads, dk, dv), lambda b, c: (b, 0, 0, 0)),
                   pl.BlockSpec((1, heads, 1, dk), lambda b, c: (b, 0, 0, 0)),
                   pl.BlockSpec((1, heads, 1, LANES), lambda b, c: (b, 0, 0, 0))],
        out_shape=[jax.ShapeDtypeStruct(h_buf.shape, F32),
                   jax.ShapeDtypeStruct((nb, heads, dk, dv), F32),
                   jax.ShapeDtypeStruct((nb, heads, 1, dk), F32),
                   jax.ShapeDtypeStruct((nb, heads, 1, LANES), F32)],
        input_output_aliases={9: 0},
        compiler_params=_params("parallel", "arbitrary"), name="mlstm",
    )(proj, proj, proj, gates, bias, c0, n0, m0, gh, h_buf)


def _extract_topk(vals, kk, payload=None):
    rows = lax.broadcasted_iota(jnp.int32, vals.shape, 0)
    nrow = vals.shape[0]
    out_v, out_i = [], []
    for _ in range(kk):
        mx = jnp.max(vals, axis=0, keepdims=True)
        idx = jnp.min(jnp.where(vals == mx, rows, nrow), axis=0, keepdims=True)
        hit = rows == idx
        out_v.append(mx)
        if payload is None:
            out_i.append(idx)
        else:
            out_i.append(jnp.max(jnp.where(hit, payload, -1), axis=0, keepdims=True))
        vals = jnp.where(hit, NEG_INF, vals)
    return jnp.concatenate(out_v, axis=0), jnp.concatenate(out_i, axis=0)


def _peer_route_kernel(q_ref, keys_ref, e_ref, g_ref, *, heads, nk, half, kk):
    for p in range(heads):
        top = []
        for c in range(2):
            col = (2 * p + c) * half
            qpc = q_ref[:, col:col + half].astype(BF16)
            s = lax.dot_general(keys_ref[p, c].astype(BF16), qpc, _NT,
                                preferred_element_type=F32)
            top.append(_extract_topk(s, kk))
        (s1, i1), (s2, i2) = top
        t = s1.shape[1]
        cand = (s1[:, None, :] + s2[None, :, :]).reshape(kk * kk, t)
        ids = (i1[:, None, :] * nk + i2[None, :, :]).reshape(kk * kk, t)
        sc, e = _extract_topk(cand, kk, payload=ids)
        w = jnp.exp(sc - jnp.max(sc, axis=0, keepdims=True))
        e_ref[p * kk:(p + 1) * kk, :] = e
        g_ref[p * kk:(p + 1) * kk, :] = w / jnp.sum(w, axis=0, keepdims=True)


def _peer_route(qr, sub_keys, *, tt):
    n = qr.shape[0]
    heads, _, nk, half = sub_keys.shape
    kk = PEER_TOPK
    kern = functools.partial(_peer_route_kernel, heads=heads, nk=nk, half=half, kk=kk)
    return pl.pallas_call(
        kern, grid=(n // tt,),
        in_specs=[pl.BlockSpec((tt, 2 * heads * half), lambda i: (i, 0)),
                  pl.BlockSpec((heads, 2, nk, half), lambda i: (0, 0, 0, 0))],
        out_specs=[pl.BlockSpec((heads * kk, tt), lambda i: (0, i)),
                   pl.BlockSpec((heads * kk, tt), lambda i: (0, i))],
        out_shape=[jax.ShapeDtypeStruct((heads * kk, n), jnp.int32),
                   jax.ShapeDtypeStruct((heads * kk, n), F32)],
        compiler_params=_params("parallel"), name="peer_route",
    )(qr, sub_keys)


def _peer_expert_kernel(e_hbm, g_ref, h_ref, x_ref, ga_ref, u_hbm, v_hbm, y_ref,
                        e_smem, ubuf, vbuf, sem_e, sem_u, sem_v, *, tt, ns):
    i = pl.program_id(0)
    cp_e = pltpu.make_async_copy(e_hbm.at[i], e_smem, sem_e)
    cp_e.start()
    cp_e.wait()

    def fetch(t, slot):
        for j in range(ns):
            e = e_smem[t, j]
            pltpu.make_async_copy(u_hbm.at[pl.ds(e, 1)], ubuf.at[slot, pl.ds(j, 1)],
                                  sem_u.at[slot]).start()
            pltpu.make_async_copy(v_hbm.at[pl.ds(e, 1)], vbuf.at[slot, pl.ds(j, 1)],
                                  sem_v.at[slot]).start()

    def wait(slot):
        pltpu.make_async_copy(u_hbm.at[pl.ds(0, ns)], ubuf.at[slot], sem_u.at[slot]).wait()
        pltpu.make_async_copy(v_hbm.at[pl.ds(0, ns)], vbuf.at[slot], sem_v.at[slot]).wait()

    fetch(0, 0)

    def body(t, _):
        slot = t % 2

        @pl.when(t + 1 < tt)
        def _():
            fetch(t + 1, 1 - slot)

        wait(slot)
        d = h_ref.shape[1]
        hb = jnp.broadcast_to(h_ref[pl.ds(t, 1), :], (8, d)).astype(BF16)
        a = lax.dot_general(hb, ubuf[slot].astype(BF16), _NT, preferred_element_type=F32)
        w = (g_ref[pl.ds(t, 1), :] * _gelu_tanh(a)).astype(BF16)
        out = jnp.dot(w, vbuf[slot].astype(BF16), preferred_element_type=F32)
        y_ref[pl.ds(t, 1), :] = x_ref[pl.ds(t, 1), :] + ga_ref[0] * out[0:1, :]
        return 0

    lax.fori_loop(0, tt, body, 0)


def _peer_expert(e, g, h, x, ga3, gate_blk, u, v, *, group, tt):
    n, d = x.shape
    ns = e.shape[1]
    per_group = group // tt
    kern = functools.partial(_peer_expert_kernel, tt=tt, ns=ns)
    return pl.pallas_call(
        kern, grid=(n // tt,),
        in_specs=[pl.BlockSpec(memory_space=pl.ANY),
                  pl.BlockSpec((tt, ns), lambda i: (i, 0)),
                  pl.BlockSpec((tt, d), lambda i: (i, 0)),
                  pl.BlockSpec((tt, d), lambda i: (i, 0)),
                  pl.BlockSpec((1, 1, d), lambda i: (i // per_group, 0, gate_blk)),
                  pl.BlockSpec(memory_space=pl.ANY),
                  pl.BlockSpec(memory_space=pl.ANY)],
        out_specs=pl.BlockSpec((tt, d), lambda i: (i, 0)),
        out_shape=jax.ShapeDtypeStruct((n, d), F32),
        scratch_shapes=[pltpu.SMEM((tt, ns), jnp.int32),
                        pltpu.VMEM((2, ns, d), F32), pltpu.VMEM((2, ns, d), F32),
                        pltpu.SemaphoreType.DMA(()), pltpu.SemaphoreType.DMA((2,)),
                        pltpu.SemaphoreType.DMA((2,))],
        compiler_params=_params("arbitrary"), name="peer_expert",
    )(e.reshape(n // tt, tt, ns), g, h, x, ga3, u, v)


def _final_norm_kernel(x_ref, g_ref, o_ref):
    x = x_ref[...]
    o_ref[...] = x * lax.rsqrt(jnp.mean(x * x, axis=-1, keepdims=True) + EPS) * g_ref[...]


def _final_norm(x, g, *, tm):
    n, d = x.shape
    return pl.pallas_call(
        _final_norm_kernel, grid=(n // tm,),
        in_specs=[pl.BlockSpec((tm, d), lambda i: (i, 0)), pl.BlockSpec((1, d), lambda i: (0, 0))],
        out_specs=pl.BlockSpec((tm, d), lambda i: (i, 0)),
        out_shape=jax.ShapeDtypeStruct((n, d), F32),
        compiler_params=_params("parallel"), name="final_norm",
    )(x, g.reshape(1, d))


def _pad_lanes(a):
    return jnp.pad(a, [(0, 0)] * (a.ndim - 1) + [(0, LANES - a.shape[-1])])


def kernel(x_prompt, x_sample, c_prompt, c_sample, cache_fox_k, cache_fox_v, cache_fox_logf, state_mlstm_C, state_mlstm_n, state_mlstm_m, w_ada, b_ada, g_norm_mix, g_norm_ffn, w_in_fox, b_f_fox, g_q_fox, g_k_fox, w_out_fox, w_in_mlstm, b_i_mlstm, b_f_mlstm, g_h_mlstm, w_out_mlstm, w_query_peer, sub_keys_peer, u_peer, v_peer, g_final):
    bp, tp, d = x_prompt.shape
    bs, ts, _ = x_sample.shape
    depth = w_ada.shape[0]
    n_p, n_s = bp * tp, bs * ts
    n = n_p + n_s
    p_len = cache_fox_k.shape[2]
    dh = g_q_fox.shape[-1]
    fh = d // dh
    mh = b_i_mlstm.shape[-1]
    dv = d // mh
    dk = (w_in_mlstm.shape[-1] - 2 * d - 2 * mh) // (2 * mh)
    group = math.gcd(tp, ts)
    assert group % 8 == 0 and fh <= LANES and 2 * mh <= LANES
    tm = min(512, n_s)
    assert tm % group == 0 and (tm // group) % 8 == 0 and n_p % tm == 0 and n_s % tm == 0
    tn = 512
    tq = min(512, tp)
    tk = min(512, p_len)
    tt_route = LANES
    tt_exp = min(32, group)

    x = jnp.concatenate([x_prompt.reshape(n_p, d), x_sample.reshape(n_s, d)], axis=0)

    nrow = bp + bs
    rpad = -(-nrow // 8) * 8
    c_all = jnp.pad(jnp.concatenate([c_prompt, c_sample], axis=0), ((0, rpad - nrow), (0, 0)))
    mod = _ada(c_all, w_ada, b_ada)
    gidx = jnp.concatenate([jnp.repeat(jnp.arange(bp), tp // group),
                            bp + jnp.repeat(jnp.arange(bs), ts // group)])
    mod_g = jnp.take(mod, gidx, axis=1)

    fox_out = []
    rec_out = []
    for i in range(depth):
        jl = i // 2
        mg = mod_g[i]
        if i % 2 == 0:
            w_in = w_in_fox[jl]
            wg = _pad_lanes(w_in[:, 4 * d:])
            gvec = jnp.concatenate([jnp.tile(g_q_fox[jl], fh), jnp.tile(g_k_fox[jl], fh),
                                    jnp.ones((2 * d,), F32)]).reshape(1, 4 * d)
            proj, gates = _norm_proj(x, g_norm_mix[i], mg, 0, 1, w_in, group=group, tm=tm, tn=tn,
                                     mode="fox", wg=wg, gvec=gvec, n_norm=2 * d // tn, head_dim=dh,
                                     ncols=4 * d)
            b_pad = _pad_lanes(b_f_fox[jl].reshape(1, fh))
            logf_p, f_p = _fox_gates(gates, b_pad, None, nb=bp, t_len=tp, row0=0)
            logf_s, f_s = _fox_gates(gates, b_pad, _pad_lanes(cache_fox_logf[jl]), nb=bs, t_len=ts,
                                     row0=n_p)
            f_p_rows = jnp.transpose(f_p[:, :, :fh], (0, 2, 1))[:, :, None, :]
            f_s_rows = jnp.transpose(f_s[:, :, :fh], (0, 2, 1))[:, :, None, :]
            o = _fox_attn_prompt(proj, f_p_rows, n, nb=bp, t_len=tp, heads=fh, dh=dh, tq=tq)
            o = _fox_attn_sample(proj, cache_fox_k.reshape(cache_fox_k.shape[:3] + (d,)),
                                 cache_fox_v.reshape(cache_fox_v.shape[:3] + (d,)), jl,
                                 f_s_rows[..., :p_len], f_s_rows[..., p_len:], o,
                                 nb=bs, t_len=ts, row0=n_p, heads=fh, dh=dh, tk=tk)
            x = _out_proj(o, proj, 3, w_out_fox[jl], x, mg, 2, group=group, tm=tm, tn=tn)
            k_new = proj[:, d:2 * d]
            v_new = proj[:, 2 * d:3 * d]
            fox_out.append(((k_new[:n_p].reshape(bp, tp, fh, dh), v_new[:n_p].reshape(bp, tp, fh, dh),
                             logf_p[:, :, :fh]),
                            (k_new[n_p:].reshape(bs, ts, fh, dh), v_new[n_p:].reshape(bs, ts, fh, dh),
                             logf_s[:, :, :fh])))
        else:
            w_in = w_in_mlstm[jl]
            qk = mh * dk
            nmain = 2 * qk + 2 * d
            wg = _pad_lanes(w_in[:, nmain:])
            gvec = jnp.concatenate([jnp.ones((qk,), F32), jnp.full((qk,), dk ** -0.5, F32),
                                    jnp.ones((2 * d,), F32)]).reshape(1, nmain)
            proj, gates = _norm_proj(x, g_norm_mix[i], mg, 0, 1, w_in, group=group, tm=tm, tn=tn,
                                     mode="scale", wg=wg, gvec=gvec, ncols=nmain)
            bias = _pad_lanes(jnp.concatenate([b_i_mlstm[jl], b_f_mlstm[jl]]).reshape(1, 2 * mh))
            c0 = jnp.concatenate([jnp.zeros((bp,) + state_mlstm_C.shape[2:], F32), state_mlstm_C[jl]])
            n0 = jnp.concatenate([jnp.zeros((bp, mh, dk), F32), state_mlstm_n[jl]])[:, :, None, :]
            m0 = jnp.concatenate([jnp.zeros((bp, mh), F32), state_mlstm_m[jl]])
            m0 = jnp.broadcast_to(m0[:, :, None, None], (bp + bs, mh, 1, LANES))
            gh = g_h_mlstm[jl].reshape(1, d)
            hbuf = jnp.zeros((n, d), F32)
            hbuf, c_p, n_pr, m_p = _mlstm(proj, gates, bias, c0, n0, m0, gh, hbuf, nb=bp, t_len=tp,
                                          row0=0, b0=0, heads=mh, dk=dk, dv=dv)
            hbuf, c_s, n_sm, m_s = _mlstm(proj, gates, bias, c0, n0, m0, gh, hbuf, nb=bs, t_len=ts,
                                          row0=n_p, b0=bp, heads=mh, dk=dk, dv=dv)
            x = _out_proj(hbuf, proj, nmain // d - 1, w_out_mlstm[jl], x, mg, 2, group=group, tm=tm,
                          tn=tn)
            rec_out.append(((c_p, n_pr[:, :, 0, :], m_p[:, :, 0, 0]),
                            (c_s, n_sm[:, :, 0, :], m_s[:, :, 0, 0])))

        qr, hf = _norm_proj(x, g_norm_ffn[i], mg, 3, 4, w_query_peer[i], group=group, tm=tm, tn=tn,
                            mode="plain")
        e_t, g_t = _peer_route(qr, sub_keys_peer[i], tt=tt_route)
        x = _peer_expert(e_t.T, g_t.T, hf, x, mg[:, None, :], 5, u_peer[i], v_peer[i],
                         group=group, tt=tt_exp)

    y = _final_norm(x, g_final, tm=tm)
    y_prompt = y[:n_p].reshape(bp, tp, d)
    y_sample = y[n_p:].reshape(bs, ts, d)
    stack = lambda grp, idx, items: jnp.stack([t[grp][idx] for t in items])
    return (y_prompt, y_sample,
            stack(0, 0, fox_out), stack(0, 1, fox_out), stack(0, 2, fox_out),
            stack(0, 0, rec_out), stack(0, 1, rec_out), stack(0, 2, rec_out),
            stack(1, 0, fox_out), stack(1, 1, fox_out), stack(1, 2, fox_out),
            stack(1, 0, rec_out), stack(1, 1, rec_out), stack(1, 2, rec_out))
```

```python
import functools
import math

import jax
import jax.numpy as jnp
from jax import lax
from jax.experimental import pallas as pl
from jax.experimental.pallas import tpu as pltpu

F32 = jnp.float32
BF16 = jnp.bfloat16
EPS = 1e-6
PEER_TOPK = 16
MLSTM_CHUNK = 64
LANES = 128
VMEM_LIMIT = 56 * 1024 * 1024
HIGHEST = lax.Precision.HIGHEST
NEG_INF = float("-inf")

_NT = (((1,), (1,)), ((), ()))
_TN = (((0,), (0,)), ((), ()))


def _params(*sem):
    return pltpu.CompilerParams(dimension_semantics=sem, vmem_limit_bytes=VMEM_LIMIT)


def _log_sigmoid(z):
    return jnp.minimum(z, 0.0) - jnp.log(1.0 + jnp.exp(-jnp.abs(z)))


def _gelu_tanh(a):
    c = math.sqrt(2.0 / math.pi)
    return 0.5 * a * (1.0 + jnp.tanh(c * (a + 0.044715 * a * a * a)))


def _ada_kernel(c_ref, w_ref, b_ref, o_ref):
    c = c_ref[...]
    s = c * jax.nn.sigmoid(c)
    o_ref[0] = jnp.dot(s, w_ref[0], precision=HIGHEST, preferred_element_type=F32) + b_ref[0]


def _ada(c_all, w_ada, b_ada):
    depth, d, n6 = w_ada.shape
    r = c_all.shape[0]
    tn = min(1024, n6)
    return pl.pallas_call(
        _ada_kernel,
        grid=(depth, n6 // tn),
        in_specs=[pl.BlockSpec((r, d), lambda l, j: (0, 0)),
                  pl.BlockSpec((1, d, tn), lambda l, j: (l, 0, j)),
                  pl.BlockSpec((1, 1, tn), lambda l, j: (l, 0, j))],
        out_specs=pl.BlockSpec((1, r, tn), lambda l, j: (l, 0, j)),
        out_shape=jax.ShapeDtypeStruct((depth, r, n6), F32),
        compiler_params=_params("parallel", "parallel"),
        name="ada",
    )(c_all, w_ada, b_ada.reshape(depth, 1, n6))


def _norm_proj_kernel(*refs, mode, group, n_norm, head_dim, has_gates, emit_h):
    x_ref, g_ref, sh_ref, sc_ref, w_ref = refs[:5]
    pos = 5
    wg_ref = gvec_ref = None
    if has_gates:
        wg_ref = refs[pos]; pos += 1
    if mode != "plain":
        gvec_ref = refs[pos]; pos += 1
    o_ref = refs[pos]; pos += 1
    gate_ref = hout_ref = None
    if has_gates:
        gate_ref = refs[pos]; pos += 1
    if emit_h:
        hout_ref = refs[pos]; pos += 1
    hs_ref = refs[pos]

    j = pl.program_id(1)
    tm = x_ref.shape[0]

    @pl.when(j == 0)
    def _():
        g = g_ref[...]
        for r in range(tm // group):
            rows = slice(r * group, (r + 1) * group)
            x = x_ref[rows, :]
            y = x * lax.rsqrt(jnp.mean(x * x, axis=-1, keepdims=True) + EPS) * g
            h = y * (1.0 + sc_ref[r:r + 1, :]) + sh_ref[r:r + 1, :]
            hs_ref[rows, :] = h.astype(BF16)
            if emit_h:
                hout_ref[rows, :] = h
            if has_gates:
                gate_ref[rows, :] = jnp.dot(h, wg_ref[...], precision=HIGHEST,
                                            preferred_element_type=F32)

    acc = jnp.dot(hs_ref[...], w_ref[...].astype(BF16), preferred_element_type=F32)
    if mode == "plain":
        o_ref[...] = acc
    elif mode == "scale":
        o_ref[...] = acc * gvec_ref[...]
    else:
        @pl.when(j < n_norm)
        def _():
            for hh in range(acc.shape[1] // head_dim):
                cs = slice(hh * head_dim, (hh + 1) * head_dim)
                seg = acc[:, cs]
                ms = jnp.mean(seg * seg, axis=-1, keepdims=True)
                o_ref[:, cs] = seg * lax.rsqrt(ms + EPS) * gvec_ref[:, cs]

        @pl.when(j >= n_norm)
        def _():
            o_ref[...] = acc


def _norm_proj(x, g, mod_g, shift_blk, scale_blk, w, *, group, tm, tn, mode,
               wg=None, gvec=None, n_norm=0, head_dim=LANES, ncols=None):
    n, d = x.shape
    ncols = w.shape[1] if ncols is None else ncols
    has_gates = wg is not None
    emit_h = mode == "plain"
    gr = tm // group
    in_specs = [pl.BlockSpec((tm, d), lambda i, j: (i, 0)),
                pl.BlockSpec((1, d), lambda i, j: (0, 0)),
                pl.BlockSpec((gr, d), lambda i, j: (i, shift_blk)),
                pl.BlockSpec((gr, d), lambda i, j: (i, scale_blk)),
                pl.BlockSpec((d, tn), lambda i, j: (0, j))]
    args = [x, g.reshape(1, d), mod_g, mod_g, w]
    if has_gates:
        in_specs.append(pl.BlockSpec((d, LANES), lambda i, j: (0, 0)))
        args.append(wg)
    if mode != "plain":
        in_specs.append(pl.BlockSpec((1, tn), lambda i, j: (0, j)))
        args.append(gvec)
    out_specs = [pl.BlockSpec((tm, tn), lambda i, j: (i, j))]
    out_shape = [jax.ShapeDtypeStruct((n, ncols), F32)]
    if has_gates:
        out_specs.append(pl.BlockSpec((tm, LANES), lambda i, j: (i, 0)))
        out_shape.append(jax.ShapeDtypeStruct((n, LANES), F32))
    if emit_h:
        out_specs.append(pl.BlockSpec((tm, d), lambda i, j: (i, 0)))
        out_shape.append(jax.ShapeDtypeStruct((n, d), F32))
    kern = functools.partial(_norm_proj_kernel, mode=mode, group=group, n_norm=n_norm,
                             head_dim=head_dim, has_gates=has_gates, emit_h=emit_h)
    return pl.pallas_call(
        kern, grid=(n // tm, ncols // tn), in_specs=in_specs, out_specs=out_specs,
        out_shape=out_shape, scratch_shapes=[pltpu.VMEM((tm, d), BF16)],
        compiler_params=_params("parallel", "arbitrary"), name="norm_proj_" + mode,
    )(*args)


def _fox_gate_kernel(*refs, p_len, t_len, ck_p, ck_t):
    if p_len:
        z_ref, b_ref, past_ref, logf_ref, f_ref = refs
    else:
        z_ref, b_ref, logf_ref, f_ref = refs
        past_ref = None

    def tri(ck):
        r = lax.broadcasted_iota(jnp.int32, (ck, ck), 0)
        c = lax.broadcasted_iota(jnp.int32, (ck, ck), 1)
        return (r >= c).astype(F32)

    carry = jnp.zeros((1, LANES), F32)
    if p_len:
        tp = tri(ck_p)
        for c in range(p_len // ck_p):
            rows = slice(c * ck_p, (c + 1) * ck_p)
            fc = jnp.dot(tp, past_ref[0, rows, :], precision=HIGHEST,
                         preferred_element_type=F32) + carry
            f_ref[0, rows, :] = fc
            carry = fc[ck_p - 1:ck_p, :]
    tt = tri(ck_t)
    for c in range(t_len // ck_t):
        rows = slice(c * ck_t, (c + 1) * ck_t)
        lf = _log_sigmoid(z_ref[rows, :] + b_ref[...])
        logf_ref[0, rows, :] = lf
        fc = jnp.dot(tt, lf, precision=HIGHEST, preferred_element_type=F32) + carry
        f_ref[0, p_len + c * ck_t:p_len + (c + 1) * ck_t, :] = fc
        carry = fc[ck_t - 1:ck_t, :]


def _fox_gates(gates, b_pad, past, *, nb, t_len, row0):
    p_len = 0 if past is None else past.shape[1]
    ck_t = min(256, t_len)
    ck_p = min(256, p_len) if p_len else 0
    blk0 = row0 // t_len
    in_specs = [pl.BlockSpec((t_len, LANES), lambda b: (blk0 + b, 0)),
                pl.BlockSpec((1, LANES), lambda b: (0, 0))]
    args = [gates, b_pad]
    if p_len:
        in_specs.append(pl.BlockSpec((1, p_len, LANES), lambda b: (b, 0, 0)))
        args.append(past)
    kern = functools.partial(_fox_gate_kernel, p_len=p_len, t_len=t_len, ck_p=ck_p, ck_t=ck_t)
    return pl.pallas_call(
        kern, grid=(nb,), in_specs=in_specs,
        out_specs=[pl.BlockSpec((1, t_len, LANES), lambda b: (b, 0, 0)),
                   pl.BlockSpec((1, p_len + t_len, LANES), lambda b: (b, 0, 0))],
        out_shape=[jax.ShapeDtypeStruct((nb, t_len, LANES), F32),
                   jax.ShapeDtypeStruct((nb, p_len + t_len, LANES), F32)],
        compiler_params=_params("parallel"), name="fox_gates",
    )(*args)


def _flash_update(s, v, m_prev, l_prev, acc_prev):
    m_new = jnp.maximum(m_prev, jnp.max(s, axis=-1, keepdims=True))
    p = jnp.exp(s - m_new)
    alpha = jnp.exp(m_prev - m_new)
    l_new = alpha * l_prev + jnp.sum(p, axis=-1, keepdims=True)
    acc_new = alpha * acc_prev + jnp.dot(p.astype(BF16), v, preferred_element_type=F32)
    return m_new, l_new, acc_new


def _fox_attn_prompt_kernel(q_ref, k_ref, v_ref, f_ref, oin_ref, o_ref, *, scale, tq, tk):
    del oin_ref
    qi = pl.program_id(2)
    q = (q_ref[...] * scale).astype(BF16)
    dh = q.shape[1]
    nsub = tq // tk

    def step(kj, carry, diag):
        m, l, acc = carry
        start = pl.multiple_of(kj * tk, tk)
        k = k_ref[pl.ds(start, tk), :].astype(BF16)
        v = v_ref[pl.ds(start, tk), :].astype(BF16)
        s = lax.dot_general(q, k, _NT, preferred_element_type=F32)
        s = s - f_ref[0, 0, kj]
        if diag is not None:
            r = lax.broadcasted_iota(jnp.int32, s.shape, 0)
            c = lax.broadcasted_iota(jnp.int32, s.shape, 1) + diag * tk
            s = jnp.where(c <= r, s, NEG_INF)
        return _flash_update(s, v, m, l, acc)

    init = (jnp.full((tq, 1), NEG_INF, F32), jnp.zeros((tq, 1), F32), jnp.zeros((tq, dh), F32))
    carry = lax.fori_loop(0, qi * nsub, lambda kj, c: step(kj, c, None), init)
    for jj in range(nsub):
        carry = step(qi * nsub + jj, carry, jj)
    m, l, acc = carry
    o_ref[...] = acc / l


def _fox_attn_prompt(proj, f_rows, n_total, *, nb, t_len, heads, dh, tq):
    d = heads * dh
    nq = t_len // tq
    tk = min(256, tq)
    f_rows = f_rows.reshape(nb, heads, t_len // tk, 1, tk)
    kern = functools.partial(_fox_attn_prompt_kernel, scale=dh ** -0.5, tq=tq, tk=tk)
    return pl.pallas_call(
        kern, grid=(nb, heads, nq),
        in_specs=[pl.BlockSpec((tq, dh), lambda b, h, i: (b * nq + i, h)),
                  pl.BlockSpec((t_len, dh), lambda b, h, i: (b, heads + h)),
                  pl.BlockSpec((t_len, dh), lambda b, h, i: (b, 2 * heads + h)),
                  pl.BlockSpec((1, 1, t_len // tk, 1, tk), lambda b, h, i: (b, h, 0, 0, 0)),
                  pl.BlockSpec(memory_space=pl.ANY)],
        out_specs=pl.BlockSpec((tq, dh), lambda b, h, i: (b * nq + i, h)),
        out_shape=jax.ShapeDtypeStruct((n_total, d), F32),
        input_output_aliases={4: 0},
        compiler_params=_params("parallel", "parallel", "arbitrary"), name="fox_attn_prompt",
    )(proj, proj, proj, f_rows, jnp.zeros((n_total, d), F32))


def _fox_attn_sample_kernel(q_ref, kn_ref, vn_ref, kp_ref, vp_ref, fp_ref, fn_ref, oin_ref,
                            o_ref, m_sc, l_sc, acc_sc, *, scale, heads, dh):
    del oin_ref
    kb = pl.program_id(1)

    @pl.when(kb == 0)
    def _():
        m_sc[...] = jnp.full(m_sc.shape, NEG_INF, F32)
        l_sc[...] = jnp.zeros(l_sc.shape, F32)
        acc_sc[...] = jnp.zeros(acc_sc.shape, F32)

    def update(h, s, v):
        m, l, acc = _flash_update(s, v, m_sc[h], l_sc[h], acc_sc[h])
        m_sc[h] = m
        l_sc[h] = l
        acc_sc[h] = acc

    for h in range(heads):
        cs = slice(h * dh, (h + 1) * dh)
        q = (q_ref[:, cs] * scale).astype(BF16)
        k = kp_ref[0, 0, :, cs].astype(BF16)
        v = vp_ref[0, 0, :, cs].astype(BF16)
        s = lax.dot_general(q, k, _NT, preferred_element_type=F32) - fp_ref[0, h]
        update(h, s, v)

    @pl.when(kb == pl.num_programs(1) - 1)
    def _():
        for h in range(heads):
            cs = slice(h * dh, (h + 1) * dh)
            q = (q_ref[:, cs] * scale).astype(BF16)
            k = kn_ref[:, cs].astype(BF16)
            v = vn_ref[:, cs].astype(BF16)
            s = lax.dot_general(q, k, _NT, preferred_element_type=F32) - fn_ref[0, h]
            r = lax.broadcasted_iota(jnp.int32, s.shape, 0)
            c = lax.broadcasted_iota(jnp.int32, s.shape, 1)
            update(h, jnp.where(c <= r, s, NEG_INF), v)
            o_ref[:, cs] = acc_sc[h] / l_sc[h]


def _fox_attn_sample(proj, cache_k, cache_v, layer, fp_rows, fn_rows, o_buf, *, nb, t_len,
                     row0, heads, dh, tk):
    d = heads * dh
    p_len = cache_k.shape[2]
    blk0 = row0 // t_len
    kern = functools.partial(_fox_attn_sample_kernel, scale=dh ** -0.5, heads=heads, dh=dh)
    return pl.pallas_call(
        kern, grid=(nb, p_len // tk),
        in_specs=[pl.BlockSpec((t_len, d), lambda b, kb: (blk0 + b, 0)),
                  pl.BlockSpec((t_len, d), lambda b, kb: (blk0 + b, 1)),
                  pl.BlockSpec((t_len, d), lambda b, kb: (blk0 + b, 2)),
                  pl.BlockSpec((1, 1, tk, d), lambda b, kb: (layer, b, kb, 0)),
                  pl.BlockSpec((1, 1, tk, d), lambda b, kb: (layer, b, kb, 0)),
                  pl.BlockSpec((1, heads, 1, tk), lambda b, kb: (b, 0, 0, kb)),
                  pl.BlockSpec((1, heads, 1, t_len), lambda b, kb: (b, 0, 0, 0)),
                  pl.BlockSpec(memory_space=pl.ANY)],
        out_specs=pl.BlockSpec((t_len, d), lambda b, kb: (blk0 + b, 0)),
        out_shape=jax.ShapeDtypeStruct(o_buf.shape, F32),
        scratch_shapes=[pltpu.VMEM((heads, t_len, 1), F32), pltpu.VMEM((heads, t_len, 1), F32),
                        pltpu.VMEM((heads, t_len, dh), F32)],
        input_output_aliases={7: 0},
        compiler_params=_params("parallel", "arbitrary"), name="fox_attn_sample",
    )(proj, proj, proj, cache_k, cache_v, fp_rows, fn_rows, o_buf)


def _out_proj_kernel(o_ref, op_ref, w_ref, x_ref, ga_ref, y_ref, lhs_ref, *, group):
    @pl.when(pl.program_id(1) == 0)
    def _():
        lhs_ref[...] = (o_ref[...] * jax.nn.sigmoid(op_ref[...])).astype(BF16)

    y = jnp.dot(lhs_ref[...], w_ref[...].astype(BF16), preferred_element_type=F32)
    tm = y.shape[0]
    for r in range(tm // group):
        rows = slice(r * group, (r + 1) * group)
        y_ref[rows, :] = x_ref[rows, :] + ga_ref[r:r + 1, :] * y[rows, :]


def _out_proj(o, proj, opre_blk, w, x, mod_g, gate_blk, *, group, tm, tn):
    n, d = x.shape
    gr = tm // group
    nj = d // tn
    return pl.pallas_call(
        functools.partial(_out_proj_kernel, group=group), grid=(n // tm, nj),
        in_specs=[pl.BlockSpec((tm, d), lambda i, j: (i, 0)),
                  pl.BlockSpec((tm, d), lambda i, j: (i, opre_blk)),
                  pl.BlockSpec((d, tn), lambda i, j: (0, j)),
                  pl.BlockSpec((tm, tn), lambda i, j: (i, j)),
                  pl.BlockSpec((gr, tn), lambda i, j: (i, gate_blk * nj + j))],
        out_specs=pl.BlockSpec((tm, tn), lambda i, j: (i, j)),
        out_shape=jax.ShapeDtypeStruct((n, d), F32),
        scratch_shapes=[pltpu.VMEM((tm, d), BF16)],
        compiler_params=_params("parallel", "arbitrary"), name="out_proj",
    )(o, proj, w, x, mod_g)


def _mlstm_kernel(q_ref, k_ref, v_ref, gt_ref, bias_ref, c0_ref, n0_ref, m0_ref, gh_ref, hin_ref,
                  hs_ref, c_ref, n_ref, m_ref, *, heads, dk, dv, chunk):
    del hin_ref

    @pl.when(pl.program_id(1) == 0)
    def _():
        c_ref[...] = c0_ref[...]
        n_ref[...] = n0_ref[...]
        m_ref[...] = m0_ref[...]

    z = gt_ref[...] + bias_ref[...]
    r = lax.broadcasted_iota(jnp.int32, (chunk, chunk), 0)
    c = lax.broadcasted_iota(jnp.int32, (chunk, chunk), 1)
    causal = r >= c
    bcum = jnp.dot(causal.astype(F32), _log_sigmoid(z), precision=HIGHEST,
                   preferred_element_type=F32)
    z_t = z.T
    bcum_t = bcum.T
    for h in range(heads):
        bcol = bcum[:, heads + h:heads + h + 1]
        brow = bcum_t[heads + h:heads + h + 1, :]
        icol = z[:, h:h + 1]
        irow = z_t[h:h + 1, :]
        m_prev = m_ref[0, h][:, :1]
        dm = jnp.where(causal, bcol - brow + irow, NEG_INF)
        inter = bcol + m_prev
        m_row = jnp.maximum(inter, jnp.max(dm, axis=-1, keepdims=True))
        qh = q_ref[:, h * dk:(h + 1) * dk]
        kh = k_ref[:, h * dk:(h + 1) * dk]
        vh = v_ref[:, h * dv:(h + 1) * dv].astype(BF16)
        qb = qh.astype(BF16)
        s = lax.dot_general(qb, kh.astype(BF16), _NT, preferred_element_type=F32) * jnp.exp(dm - m_row)
        decay = jnp.exp(inter - m_row)
        c_old = c_ref[0, h]
        n_old = n_ref[0, h]
        num = (jnp.dot(s.astype(BF16), vh, preferred_element_type=F32)
               + decay * jnp.dot(qb, c_old.astype(BF16), preferred_element_type=F32))
        den = (jnp.sum(s, axis=-1, keepdims=True)
               + decay * jnp.sum(qb.astype(F32) * n_old.astype(BF16).astype(F32), axis=-1, keepdims=True))
        hc = num / jnp.maximum(jnp.abs(den), jnp.exp(-m_row))
        blast = bcol[chunk - 1:chunk, :]
        m_new = m_row[chunk - 1:chunk, :]
        wend = jnp.exp(blast - bcol + icol - m_new)
        g = jnp.exp(blast + m_prev - m_new)
        kw = kh * wend
        c_ref[0, h] = g * c_old + lax.dot_general(kw.astype(BF16), vh, _TN, preferred_element_type=F32)
        n_ref[0, h] = g * n_old + jnp.sum(kw, axis=0, keepdims=True)
        m_ref[0, h] = jnp.broadcast_to(m_new, (1, LANES))
        hn = hc * lax.rsqrt(jnp.mean(hc * hc, axis=-1, keepdims=True) + EPS)
        hs_ref[:, h * dv:(h + 1) * dv] = hn * gh_ref[:, h * dv:(h + 1) * dv]


def _mlstm(proj, gates, bias, c0, n0, m0, gh, h_buf, *, nb, t_len, row0, b0, heads, dk, dv):
    d = heads * dv
    chunk = min(MLSTM_CHUNK, t_len)
    nc = t_len // chunk
    blk0 = row0 // chunk
    qk = heads * dk
    vblk = 2 * qk // d
    kern = functools.partial(_mlstm_kernel, heads=heads, dk=dk, dv=dv, chunk=chunk)
    row = lambda b, c: blk0 + b * nc + c
    return pl.pallas_call(
        kern, grid=(nb, nc),
        in_specs=[pl.BlockSpec((chunk, qk), lambda b, c: (row(b, c), 0)),
                  pl.BlockSpec((chunk, qk), lambda b, c: (row(b, c), 1)),
                  pl.BlockSpec((chunk, d), lambda b, c: (row(b, c), vblk)),
                  pl.BlockSpec((chunk, LANES), lambda b, c: (row(b, c), 0)),
                  pl.BlockSpec((1, LANES), lambda b, c: (0, 0)),
                  pl.BlockSpec((1, heads, dk, dv), lambda b, c: (b0 + b, 0, 0, 0)),
                  pl.BlockSpec((1, heads, 1, dk), lambda b, c: (b0 + b, 0, 0, 0)),
                  pl.BlockSpec((1, heads, 1, LANES), lambda b, c: (b0 + b, 0, 0, 0)),
                  pl.BlockSpec((1, d), lambda b, c: (0, 0)),
                  pl.BlockSpec(memory_space=pl.ANY)],
        out_specs=[pl.BlockSpec((chunk, d), lambda b, c: (row(b, c), 0)),
                   pl.BlockSpec((1, heads, dk, dv), lambda b, c: (b, 0, 0, 0)),
                   pl.BlockSpec((1, heads, 1, dk), lambda b, c: (b, 0, 0, 0)),
                   pl.BlockSpec((1, heads, 1, LANES), lambda b, c: (b, 0, 0, 0))],
        out_shape=[jax.ShapeDtypeStruct(h_buf.shape, F32),
                   jax.ShapeDtypeStruct((nb, heads, dk, dv), F32),
                   jax.ShapeDtypeStruct((nb, heads, 1, dk), F32),
                   jax.ShapeDtypeStruct((nb, heads, 1, LANES), F32)],
        input_output_aliases={9: 0},
        compiler_params=_params("parallel", "arbitrary"), name="mlstm",
    )(proj, proj, proj, gates, bias, c0, n0, m0, gh, h_buf)


def _extract_topk(vals, kk, payload=None):
    rows = lax.broadcasted_iota(jnp.int32, vals.shape, 0)
    nrow = vals.shape[0]
    out_v, out_i = [], []
    for _ in range(kk):
        mx = jnp.max(vals, axis=0, keepdims=True)
        idx = jnp.min(jnp.where(vals == mx, rows, nrow), axis=0, keepdims=True)
        hit = rows == idx
        out_v.append(mx)
        if payload is None:
            out_i.append(idx)
        else:
            out_i.append(jnp.max(jnp.where(hit, payload, -1), axis=0, keepdims=True))
        vals = jnp.where(hit, NEG_INF, vals)
    return jnp.concatenate(out_v, axis=0), jnp.concatenate(out_i, axis=0)


def _peer_route_kernel(q_ref, keys_ref, e_ref, g_ref, *, heads, nk, half, kk):
    for p in range(heads):
        top = []
        for c in range(2):
            col = (2 * p + c) * half
            qpc = q_ref[:, col:col + half].astype(BF16)
            s = lax.dot_general(keys_ref[p, c].astype(BF16), qpc, _NT,
                                preferred_element_type=F32)
            top.append(_extract_topk(s, kk))
        (s1, i1), (s2, i2) = top
        t = s1.shape[1]
        sub = lax.broadcasted_iota(jnp.int32, (8, t), 0)
        cand_parts, id_parts = [], []
        a_tail = kk // 2
        for a in range(a_tail):
            nb = kk // (a + 1)
            for r0 in range(0, nb, 8):
                cv = s1[a:a + 1] + s2[r0:r0 + 8]
                if nb - r0 < 8:
                    cv = jnp.where(sub < nb - r0, cv, NEG_INF)
                cand_parts.append(cv)
                id_parts.append(i1[a:a + 1] * nk + i2[r0:r0 + 8])
        cand_parts.append(s1[a_tail:] + s2[0:1])
        id_parts.append(i1[a_tail:] * nk + i2[0:1])
        cand = jnp.concatenate(cand_parts, axis=0)
        ids = jnp.concatenate(id_parts, axis=0)
        sc, e = _extract_topk(cand, kk, payload=ids)
        w = jnp.exp(sc - jnp.max(sc, axis=0, keepdims=True))
        e_ref[p * kk:(p + 1) * kk, :] = e
        g_ref[p * kk:(p + 1) * kk, :] = w / jnp.sum(w, axis=0, keepdims=True)


def _peer_route(qr, sub_keys, *, tt):
    n = qr.shape[0]
    heads, _, nk, half = sub_keys.shape
    kk = PEER_TOPK
    kern = functools.partial(_peer_route_kernel, heads=heads, nk=nk, half=half, kk=kk)
    return pl.pallas_call(
        kern, grid=(n // tt,),
        in_specs=[pl.BlockSpec((tt, 2 * heads * half), lambda i: (i, 0)),
                  pl.BlockSpec((heads, 2, nk, half), lambda i: (0, 0, 0, 0))],
        out_specs=[pl.BlockSpec((heads * kk, tt), lambda i: (0, i)),
                   pl.BlockSpec((heads * kk, tt), lambda i: (0, i))],
        out_shape=[jax.ShapeDtypeStruct((heads * kk, n), jnp.int32),
                   jax.ShapeDtypeStruct((heads * kk, n), F32)],
        compiler_params=_params("parallel"), name="peer_route",
    )(qr, sub_keys)


PEER_SLOTS = 3


def _pack_uv(u, v):
    ub = lax.bitcast_convert_type(u.astype(BF16), jnp.uint16).astype(jnp.uint32)
    vb = lax.bitcast_convert_type(v.astype(BF16), jnp.uint16).astype(jnp.uint32)
    return ub | (vb << 16)


def _peer_expert_kernel(e_hbm, g_ref, h_ref, x_ref, ga_ref, uv_hbm, y_ref,
                        e_smem, buf, sem_e, sem, *, tt, ns):
    i = pl.program_id(0)
    cp_e = pltpu.make_async_copy(e_hbm.at[i], e_smem, sem_e)
    cp_e.start()
    cp_e.wait()

    def row_copy(e, slot, j):
        return pltpu.make_async_copy(uv_hbm.at[pl.ds(e, 1)], buf.at[slot, pl.ds(j, 1)], sem.at[slot])

    def fetch(t, slot):
        for j in range(ns):
            row_copy(e_smem[t, j], slot, j).start()

    def wait(slot):
        pltpu.make_async_copy(uv_hbm.at[pl.ds(0, ns)], buf.at[slot], sem.at[slot]).wait()

    ahead = PEER_SLOTS - 1
    for t0 in range(ahead):
        fetch(t0, t0)

    def body(t, _):
        slot = t % PEER_SLOTS

        @pl.when(t + ahead < tt)
        def _():
            fetch(t + ahead, (t + ahead) % PEER_SLOTS)

        wait(slot)
        d = h_ref.shape[1]
        bits = buf[slot]
        ub = lax.bitcast_convert_type(bits << 16, F32).astype(BF16)
        vb = lax.bitcast_convert_type(bits & jnp.uint32(0xFFFF0000), F32).astype(BF16)
        hb = jnp.broadcast_to(h_ref[pl.ds(t, 1), :], (8, d)).astype(BF16)
        a = lax.dot_general(hb, ub, _NT, preferred_element_type=F32)
        w = (g_ref[pl.ds(t, 1), :] * _gelu_tanh(a)).astype(BF16)
        out = jnp.dot(w, vb, preferred_element_type=F32)
        y_ref[pl.ds(t, 1), :] = x_ref[pl.ds(t, 1), :] + ga_ref[0] * out[0:1, :]
        return 0

    lax.fori_loop(0, tt, body, 0)


def _peer_expert(e, g, h, x, ga3, gate_blk, uv, *, group, tt):
    n, d = x.shape
    ns = e.shape[1]
    assert tt >= PEER_SLOTS and group % tt == 0
    per_group = group // tt
    kern = functools.partial(_peer_expert_kernel, tt=tt, ns=ns)
    return pl.pallas_call(
        kern, grid=(n // tt,),
        in_specs=[pl.BlockSpec(memory_space=pl.ANY),
                  pl.BlockSpec((tt, ns), lambda i: (i, 0)),
                  pl.BlockSpec((tt, d), lambda i: (i, 0)),
                  pl.BlockSpec((tt, d), lambda i: (i, 0)),
                  pl.BlockSpec((1, 1, d), lambda i: (i // per_group, 0, gate_blk)),
                  pl.BlockSpec(memory_space=pl.ANY)],
        out_specs=pl.BlockSpec((tt, d), lambda i: (i, 0)),
        out_shape=jax.ShapeDtypeStruct((n, d), F32),
        scratch_shapes=[pltpu.SMEM((tt, ns), jnp.int32),
                        pltpu.VMEM((PEER_SLOTS, ns, d), jnp.uint32),
                        pltpu.SemaphoreType.DMA(()), pltpu.SemaphoreType.DMA((PEER_SLOTS,))],
        compiler_params=_params("arbitrary"), name="peer_expert",
    )(e.reshape(n // tt, tt, ns), g, h, x, ga3, uv)


def _final_norm_kernel(x_ref, g_ref, o_ref):
    x = x_ref[...]
    o_ref[...] = x * lax.rsqrt(jnp.mean(x * x, axis=-1, keepdims=True) + EPS) * g_ref[...]


def _final_norm(x, g, *, tm):
    n, d = x.shape
    return pl.pallas_call(
        _final_norm_kernel, grid=(n // tm,),
        in_specs=[pl.BlockSpec((tm, d), lambda i: (i, 0)), pl.BlockSpec((1, d), lambda i: (0, 0))],
        out_specs=pl.BlockSpec((tm, d), lambda i: (i, 0)),
        out_shape=jax.ShapeDtypeStruct((n, d), F32),
        compiler_params=_params("parallel"), name="final_norm",
    )(x, g.reshape(1, d))


def _pad_lanes(a):
    return jnp.pad(a, [(0, 0)] * (a.ndim - 1) + [(0, LANES - a.shape[-1])])


def kernel(x_prompt, x_sample, c_prompt, c_sample, cache_fox_k, cache_fox_v, cache_fox_logf, state_mlstm_C, state_mlstm_n, state_mlstm_m, w_ada, b_ada, g_norm_mix, g_norm_ffn, w_in_fox, b_f_fox, g_q_fox, g_k_fox, w_out_fox, w_in_mlstm, b_i_mlstm, b_f_mlstm, g_h_mlstm, w_out_mlstm, w_query_peer, sub_keys_peer, u_peer, v_peer, g_final):
    bp, tp, d = x_prompt.shape
    bs, ts, _ = x_sample.shape
    depth = w_ada.shape[0]
    n_p, n_s = bp * tp, bs * ts
    n = n_p + n_s
    p_len = cache_fox_k.shape[2]
    dh = g_q_fox.shape[-1]
    fh = d // dh
    mh = b_i_mlstm.shape[-1]
    dv = d // mh
    dk = (w_in_mlstm.shape[-1] - 2 * d - 2 * mh) // (2 * mh)
    group = math.gcd(tp, ts)
    assert group % 8 == 0 and fh <= LANES and 2 * mh <= LANES
    tm = min(512, n_s)
    assert tm % group == 0 and (tm // group) % 8 == 0 and n_p % tm == 0 and n_s % tm == 0
    tn = 512
    tq = min(512, tp)
    tk = min(512, p_len)
    tt_route = LANES
    tt_exp = group

    x = jnp.concatenate([x_prompt.reshape(n_p, d), x_sample.reshape(n_s, d)], axis=0)

    nrow = bp + bs
    rpad = -(-nrow // 8) * 8
    c_all = jnp.pad(jnp.concatenate([c_prompt, c_sample], axis=0), ((0, rpad - nrow), (0, 0)))
    mod = _ada(c_all, w_ada, b_ada)
    gidx = jnp.concatenate([jnp.repeat(jnp.arange(bp), tp // group),
                            bp + jnp.repeat(jnp.arange(bs), ts // group)])
    mod_g = jnp.take(mod, gidx, axis=1)

    fox_out = []
    rec_out = []
    for i in range(depth):
        jl = i // 2
        mg = mod_g[i]
        if i % 2 == 0:
            w_in = w_in_fox[jl]
            wg = _pad_lanes(w_in[:, 4 * d:])
            gvec = jnp.concatenate([jnp.tile(g_q_fox[jl], fh), jnp.tile(g_k_fox[jl], fh),
                                    jnp.ones((2 * d,), F32)]).reshape(1, 4 * d)
            proj, gates = _norm_proj(x, g_norm_mix[i], mg, 0, 1, w_in.astype(BF16), group=group, tm=tm, tn=tn,
                                     mode="fox", wg=wg, gvec=gvec, n_norm=2 * d // tn, head_dim=dh,
                                     ncols=4 * d)
            b_pad = _pad_lanes(b_f_fox[jl].reshape(1, fh))
            logf_p, f_p = _fox_gates(gates, b_pad, None, nb=bp, t_len=tp, row0=0)
            logf_s, f_s = _fox_gates(gates, b_pad, _pad_lanes(cache_fox_logf[jl]), nb=bs, t_len=ts,
                                     row0=n_p)
            f_p_rows = jnp.transpose(f_p[:, :, :fh], (0, 2, 1))[:, :, None, :]
            f_s_rows = jnp.transpose(f_s[:, :, :fh], (0, 2, 1))[:, :, None, :]
            o = _fox_attn_prompt(proj, f_p_rows, n, nb=bp, t_len=tp, heads=fh, dh=dh, tq=tq)
            o = _fox_attn_sample(proj, cache_fox_k.reshape(cache_fox_k.shape[:3] + (d,)),
                                 cache_fox_v.reshape(cache_fox_v.shape[:3] + (d,)), jl,
                                 f_s_rows[..., :p_len], f_s_rows[..., p_len:], o,
                                 nb=bs, t_len=ts, row0=n_p, heads=fh, dh=dh, tk=tk)
            x = _out_proj(o, proj, 3, w_out_fox[jl].astype(BF16), x, mg, 2, group=group, tm=tm, tn=tn)
            k_new = proj[:, d:2 * d]
            v_new = proj[:, 2 * d:3 * d]
            fox_out.append(((k_new[:n_p].reshape(bp, tp, fh, dh), v_new[:n_p].reshape(bp, tp, fh, dh),
                             logf_p[:, :, :fh]),
                            (k_new[n_p:].reshape(bs, ts, fh, dh), v_new[n_p:].reshape(bs, ts, fh, dh),
                             logf_s[:, :, :fh])))
        else:
            w_in = w_in_mlstm[jl]
            qk = mh * dk
            nmain = 2 * qk + 2 * d
            wg = _pad_lanes(w_in[:, nmain:])
            gvec = jnp.concatenate([jnp.ones((qk,), F32), jnp.full((qk,), dk ** -0.5, F32),
                                    jnp.ones((2 * d,), F32)]).reshape(1, nmain)
            proj, gates = _norm_proj(x, g_norm_mix[i], mg, 0, 1, w_in.astype(BF16), group=group, tm=tm, tn=tn,
                                     mode="scale", wg=wg, gvec=gvec, ncols=nmain)
            bias = _pad_lanes(jnp.concatenate([b_i_mlstm[jl], b_f_mlstm[jl]]).reshape(1, 2 * mh))
            c0 = jnp.concatenate([jnp.zeros((bp,) + state_mlstm_C.shape[2:], F32), state_mlstm_C[jl]])
            n0 = jnp.concatenate([jnp.zeros((bp, mh, dk), F32), state_mlstm_n[jl]])[:, :, None, :]
            m0 = jnp.concatenate([jnp.zeros((bp, mh), F32), state_mlstm_m[jl]])
            m0 = jnp.broadcast_to(m0[:, :, None, None], (bp + bs, mh, 1, LANES))
            gh = g_h_mlstm[jl].reshape(1, d)
            hbuf = jnp.zeros((n, d), F32)
            hbuf, c_p, n_pr, m_p = _mlstm(proj, gates, bias, c0, n0, m0, gh, hbuf, nb=bp, t_len=tp,
                                          row0=0, b0=0, heads=mh, dk=dk, dv=dv)
            hbuf, c_s, n_sm, m_s = _mlstm(proj, gates, bias, c0, n0, m0, gh, hbuf, nb=bs, t_len=ts,
                                          row0=n_p, b0=bp, heads=mh, dk=dk, dv=dv)
            x = _out_proj(hbuf, proj, nmain // d - 1, w_out_mlstm[jl].astype(BF16), x, mg, 2, group=group, tm=tm,
                          tn=tn)
            rec_out.append(((c_p, n_pr[:, :, 0, :], m_p[:, :, 0, 0]),
                            (c_s, n_sm[:, :, 0, :], m_s[:, :, 0, 0])))

        qr, hf = _norm_proj(x, g_norm_ffn[i], mg, 3, 4, w_query_peer[i].astype(BF16), group=group, tm=tm, tn=tn,
                            mode="plain")
        e_t, g_t = _peer_route(qr, sub_keys_peer[i], tt=tt_route)
        x = _peer_expert(e_t.T, g_t.T, hf, x, mg[:, None, :], 5, _pack_uv(u_peer[i], v_peer[i]),
                         group=group, tt=tt_exp)

    y = _final_norm(x, g_final, tm=tm)
    y_prompt = y[:n_p].reshape(bp, tp, d)
    y_sample = y[n_p:].reshape(bs, ts, d)
    stack = lambda grp, idx, items: jnp.stack([t[grp][idx] for t in items])
    return (y_prompt, y_sample,
            stack(0, 0, fox_out), stack(0, 1, fox_out), stack(0, 2, fox_out),
            stack(0, 0, rec_out), stack(0, 1, rec_out), stack(0, 2, rec_out),
            stack(1, 0, fox_out), stack(1, 1, fox_out), stack(1, 2, fox_out),
            stack(1, 0, rec_out), stack(1, 1, rec_out), stack(1, 2, rec_out))
```
